```python
import math
import jax, jax.numpy as jnp
from jax import lax
import numpy as np

D_MODEL = 2048
BATCH = 4
SEQ = 4096
DEPTH = 1
DEC_BATCH = 8
DEC_SEQ = 64
PAST_LEN = 1024

CHUNK = 64
SSD_EXPAND = 2
D_INNER = SSD_EXPAND * D_MODEL
SSD_HEAD_DIM = 64
SSD_HEADS = D_INNER // SSD_HEAD_DIM
SSD_GROUPS = 8
SSD_HPG = SSD_HEADS // SSD_GROUPS
SSD_STATE = 128
CONV_WIDTH = 4
CONV_DIM = D_INNER + 2 * SSD_GROUPS * SSD_STATE
SSD_CHUNK = CHUNK
ATT_HEADS = 16
ATT_HEAD_DIM = 128
ATT_WIDTH = ATT_HEADS * ATT_HEAD_DIM
PREV_CHUNKS = 8
BAND_ROWS = PREV_CHUNKS * CHUNK
REL_CLIP = 128
D_FF = 4 * D_MODEL
ALPHA = (2 * DEPTH) ** 0.25
BETA = (8 * DEPTH) ** -0.25
LN_EPS = 1e-5
RMS_EPS = 1e-5
NEG_INF = -1e30

OFF_XBC = D_INNER
OFF_DT = OFF_XBC + CONV_DIM
OFF_Q = OFF_DT + SSD_HEADS
OFF_K = OFF_Q + ATT_WIDTH
OFF_V = OFF_K + ATT_WIDTH
OFF_GS = OFF_V + ATT_WIDTH
OFF_GA = OFF_GS + D_MODEL
D_IN_PROJ = OFF_GA + D_MODEL

kernel_name = "hybrid_ssd_chunkband_stream_step"


def layer_norm(x, g, b):
    xf = x.astype(jnp.float32)
    mu = jnp.mean(xf, axis=-1, keepdims=True)
    xc = xf - mu
    var = jnp.mean(xc * xc, axis=-1, keepdims=True)
    return (xc * lax.rsqrt(var + LN_EPS) * g + b).astype(x.dtype)


def grouped_rms_norm(y, w):
    b, L, _ = y.shape
    yg = y.astype(jnp.float32).reshape(b, L, SSD_GROUPS, D_INNER // SSD_GROUPS)
    yg = yg * lax.rsqrt(jnp.mean(yg * yg, axis=-1, keepdims=True) + RMS_EPS)
    return yg.reshape(b, L, D_INNER) * w


def causal_dwconv(u, prev, w, bias):
    L = u.shape[1]
    up = jnp.concatenate([prev.astype(u.dtype), u], axis=1)
    y = bias + sum(up[:, k:k + L] * w[k] for k in range(CONV_WIDTH))
    return jax.nn.silu(y), up[:, -(CONV_WIDTH - 1):]


def ssd_block_step(state, blk, A):
    x, dt, Bm, Cm = blk
    Q = x.shape[1]
    cum = jnp.cumsum(dt * A, axis=1)
    tri = jnp.tril(jnp.ones((Q, Q), dtype=bool))[None, :, :, None, None]
    seg = cum[:, :, None] - cum[:, None, :]
    decay = jnp.exp(jnp.where(tri, seg, -jnp.inf))
    cb = jnp.einsum("bign,bjgn->bijg", Cm, Bm)
    y_diag = jnp.einsum("bijg,bijgh,bjgh,bjghp->bighp", cb, decay, dt, x)
    y_off = jnp.einsum("bign,bghpn,bigh->bighp", Cm, state, jnp.exp(cum))
    to_end = jnp.exp(cum[:, -1:] - cum) * dt
    new_state = state * jnp.exp(cum[:, -1])[..., None, None] + jnp.einsum(
        "bjgn,bjgh,bjghp->bghpn", Bm, to_end, x)
    return new_state, y_diag + y_off


def ssd_scan(x, dt, A, Bm, Cm, state0, q):
    b, L = x.shape[:2]
    nc = L // q

    def blocks(t):
        return jnp.moveaxis(t.reshape((b, nc, q) + t.shape[2:]), 1, 0)

    state, ys = lax.scan(lambda s, blk: ssd_block_step(s, blk, A), state0,
                         (blocks(x), blocks(dt), blocks(Bm), blocks(Cm)))
    y = jnp.moveaxis(ys, 0, 1).reshape(x.shape)
    return y, state


def rel_bias(table, qpos, kpos):
    d = qpos[:, None] - kpos[None, :]
    idx = jnp.clip(d, -REL_CLIP, REL_CLIP) + REL_CLIP
    return table[:, idx]


def attend(q, k, v, bias, valid):
    s = jnp.einsum("bqhd,bkhd->bhqk", q, k).astype(jnp.float32) * (ATT_HEAD_DIM ** -0.5)
    s = s + bias[None].astype(jnp.float32)
    if valid is not None:
        s = jnp.where(valid, s, NEG_INF)
    p = jax.nn.softmax(s, axis=-1).astype(v.dtype)
    return jnp.einsum("bhqk,bkhd->bqhd", p, v)


def chunk_band_attention_prompt(q, k, v, table):
    b, L = q.shape[:2]
    nc = L // CHUNK
    band = BAND_ROWS + CHUNK
    pad = ((0, 0), (BAND_ROWS, 0), (0, 0), (0, 0))
    kp = jnp.pad(k, pad)
    vp = jnp.pad(v, pad)
    kpos = jnp.arange(band) - BAND_ROWS
    bias = rel_bias(table, jnp.arange(CHUNK), kpos)

    def one_chunk(c):
        start = c * CHUNK
        qc = lax.dynamic_slice_in_dim(q, start, CHUNK, axis=1)
        kc = lax.dynamic_slice_in_dim(kp, start, band, axis=1)
        vc = lax.dynamic_slice_in_dim(vp, start, band, axis=1)
        valid = (kpos + start >= 0)[None, None, None, :]
        return attend(qc, kc, vc, bias, valid)

    out = lax.map(one_chunk, jnp.arange(nc))
    return jnp.moveaxis(out, 0, 1).reshape(q.shape)


def chunk_band_attention_step(q, k, v, k_cache, v_cache, table):
    L = q.shape[1]
    R = k_cache.shape[1]
    keys = jnp.concatenate([k_cache.astype(k.dtype), k], axis=1)
    vals = jnp.concatenate([v_cache.astype(v.dtype), v], axis=1)
    kpos = jnp.concatenate([jnp.arange(R) - R, jnp.arange(L)])
    bias = rel_bias(table, jnp.arange(L), kpos)
    return attend(q, keys, vals, bias, None)


def hybrid_layer(x, k_cache, v_cache, conv_prev, ssd_prev, w_in, conv_w, conv_b, dt_bias,
                 a_log, d_skip, ssd_norm_w, rel_table, w_ssd_out, w_att_out, w_out,
                 ln1_g, ln1_b, w_up, w_down, ln2_g, ln2_b):
    b, L, _ = x.shape
    prompt = k_cache is None
    proj = x @ w_in
    z = proj[..., :OFF_XBC]
    xbc = proj[..., OFF_XBC:OFF_DT]
    dt_raw = proj[..., OFF_DT:OFF_Q]
    q = proj[..., OFF_Q:OFF_K].reshape(b, L, ATT_HEADS, ATT_HEAD_DIM)
    k = proj[..., OFF_K:OFF_V].reshape(b, L, ATT_HEADS, ATT_HEAD_DIM)
    v = proj[..., OFF_V:OFF_GS].reshape(b, L, ATT_HEADS, ATT_HEAD_DIM)
    g_ssd = jax.nn.sigmoid(proj[..., OFF_GS:OFF_GA])
    g_att = jax.nn.sigmoid(proj[..., OFF_GA:])

    if prompt:
        conv_prev = jnp.zeros((b, CONV_WIDTH - 1, CONV_DIM), x.dtype)
        ssd_prev = jnp.zeros((b, SSD_HEADS, SSD_HEAD_DIM, SSD_STATE), jnp.float32)
    xbc, conv_new = causal_dwconv(xbc, conv_prev, conv_w, conv_b)
    gn = SSD_GROUPS * SSD_STATE
    xs = xbc[..., :D_INNER].reshape(b, L, SSD_GROUPS, SSD_HPG, SSD_HEAD_DIM)
    Bm = xbc[..., D_INNER:D_INNER + gn].reshape(b, L, SSD_GROUPS, SSD_STATE)
    Cm = xbc[..., D_INNER + gn:].reshape(b, L, SSD_GROUPS, SSD_STATE)
    dt = jax.nn.softplus(dt_raw.astype(jnp.float32) + dt_bias).reshape(b, L, SSD_GROUPS, SSD_HPG)
    A = -jnp.exp(a_log.astype(jnp.float32)).reshape(SSD_GROUPS, SSD_HPG)
    state0 = ssd_prev.astype(jnp.float32).reshape(b, SSD_GROUPS, SSD_HPG, SSD_HEAD_DIM, SSD_STATE)
    y, ssd_new = ssd_scan(xs, dt, A, Bm, Cm, state0, SSD_CHUNK if prompt else L)
    y = y + d_skip.reshape(SSD_GROUPS, SSD_HPG)[..., None] * xs
    y = y.reshape(b, L, D_INNER) * jax.nn.silu(z)
    y_ssd = grouped_rms_norm(y, ssd_norm_w).astype(x.dtype)
    ssd_new = ssd_new.reshape(b, SSD_HEADS, SSD_HEAD_DIM, SSD_STATE)

    if prompt:
        y_att = chunk_band_attention_prompt(q, k, v, rel_table)
        rows = min(BAND_ROWS, L)
        k_new, v_new = k[:, L - rows:], v[:, L - rows:]
    else:
        y_att = chunk_band_attention_step(q, k, v, k_cache, v_cache, rel_table)
        k_new, v_new = k, v

    merged = g_ssd * (y_ssd @ w_ssd_out) + g_att * (y_att.reshape(b, L, ATT_WIDTH) @ w_att_out)
    h = layer_norm(ALPHA * x + merged @ w_out, ln1_g, ln1_b)
    f = jnp.square(jax.nn.relu(h @ w_up)) @ w_down
    out = layer_norm(ALPHA * h + f, ln2_g, ln2_b)
    return out, k_new, v_new, conv_new, ssd_new


def setup_inputs(seed: int = 0) -> dict:
    key = jax.random.key(seed)
    ks = jax.random.split(key, 24)
    f32 = jnp.float32
    R = min(BAND_ROWS, PAST_LEN)

    def nrm(k, shape, scale):
        return jax.random.normal(k, shape, f32) * scale

    dt0 = jnp.exp(jax.random.uniform(ks[10], (DEPTH, SSD_HEADS), f32, math.log(1e-3), math.log(1e-1)))
    return {
        "x_prompt": nrm(ks[0], (BATCH, SEQ, D_MODEL), 1.0),
        "x_sample": nrm(ks[1], (DEC_BATCH, DEC_SEQ, D_MODEL), 1.0),
        "cache_k": nrm(ks[2], (DEPTH, DEC_BATCH, R, ATT_HEADS, ATT_HEAD_DIM), 1.0),
        "cache_v": nrm(ks[3], (DEPTH, DEC_BATCH, R, ATT_HEADS, ATT_HEAD_DIM), 1.0),
        "state_conv": nrm(ks[4], (DEPTH, DEC_BATCH, CONV_WIDTH - 1, CONV_DIM), 1.0),
        "state_ssd": nrm(ks[5], (DEPTH, DEC_BATCH, SSD_HEADS, SSD_HEAD_DIM, SSD_STATE), 0.1),
        "w_in": nrm(ks[6], (DEPTH, D_MODEL, D_IN_PROJ), D_MODEL ** -0.5),
        "conv_w": nrm(ks[7], (DEPTH, CONV_WIDTH, CONV_DIM), CONV_WIDTH ** -0.5),
        "conv_b": nrm(ks[8], (DEPTH, CONV_DIM), 0.02),
        "dt_bias": dt0 + jnp.log(-jnp.expm1(-dt0)),
        "a_log": jnp.log(jax.random.uniform(ks[11], (DEPTH, SSD_HEADS), f32, 1.0, 16.0)),
        "d_skip": 1.0 + nrm(ks[12], (DEPTH, SSD_HEADS), 0.1),
        "ssd_norm_w": 1.0 + nrm(ks[13], (DEPTH, D_INNER), 0.02),
        "rel_table": nrm(ks[14], (DEPTH, ATT_HEADS, 2 * REL_CLIP + 1), 0.1),
        "w_ssd_out": nrm(ks[15], (DEPTH, D_INNER, D_MODEL), D_INNER ** -0.5),
        "w_att_out": nrm(ks[16], (DEPTH, ATT_WIDTH, D_MODEL), ATT_WIDTH ** -0.5),
        "w_out": nrm(ks[17], (DEPTH, D_MODEL, D_MODEL), BETA * D_MODEL ** -0.5),
        "ln1_g": 1.0 + nrm(ks[18], (DEPTH, D_MODEL), 0.02),
        "ln1_b": nrm(ks[19], (DEPTH, D_MODEL), 0.02),
        "w_up": nrm(ks[20], (DEPTH, D_MODEL, D_FF), D_MODEL ** -0.5),
        "w_down": nrm(ks[21], (DEPTH, D_FF, D_MODEL), BETA * D_FF ** -0.5),
        "ln2_g": 1.0 + nrm(ks[22], (DEPTH, D_MODEL), 0.02),
        "ln2_b": nrm(ks[23], (DEPTH, D_MODEL), 0.02),
    }


def reference(x_prompt, x_sample, cache_k, cache_v, state_conv, state_ssd, w_in, conv_w, conv_b,
              dt_bias, a_log, d_skip, ssd_norm_w, rel_table, w_ssd_out, w_att_out, w_out,
              ln1_g, ln1_b, w_up, w_down, ln2_g, ln2_b):
    hp, hs = x_prompt, x_sample
    kp_l, vp_l, cp_l, sp_l = [], [], [], []
    ks_l, vs_l, cs_l, ss_l = [], [], [], []
    for l in range(DEPTH):
        params = (w_in[l], conv_w[l], conv_b[l], dt_bias[l], a_log[l], d_skip[l], ssd_norm_w[l],
                  rel_table[l], w_ssd_out[l], w_att_out[l], w_out[l], ln1_g[l], ln1_b[l],
                  w_up[l], w_down[l], ln2_g[l], ln2_b[l])
        hp, kp_, vp_, cp_, sp_ = hybrid_layer(hp, None, None, None, None, *params)
        hs, ks_, vs_, cs_, ss_ = hybrid_layer(hs, cache_k[l], cache_v[l], state_conv[l],
                                              state_ssd[l], *params)
        kp_l.append(kp_); vp_l.append(vp_); cp_l.append(cp_); sp_l.append(sp_)
        ks_l.append(ks_); vs_l.append(vs_); cs_l.append(cs_); ss_l.append(ss_)
    return (hp, hs,
            jnp.stack(kp_l), jnp.stack(vp_l), jnp.stack(cp_l), jnp.stack(sp_l),
            jnp.stack(ks_l), jnp.stack(vs_l), jnp.stack(cs_l), jnp.stack(ss_l))
```

```python
import functools

import jax
import jax.numpy as jnp
from jax import lax
from jax.experimental import pallas as pl
from jax.experimental.pallas import tpu as pltpu

F32 = jnp.float32
BF16 = jnp.bfloat16

D_MODEL = 2048
CHUNK = 64
D_INNER = 4096
SSD_HEAD_DIM = 64
SSD_HEADS = 64
SSD_GROUPS = 8
SSD_STATE = 128
CONV_WIDTH = 4
GN = SSD_GROUPS * SSD_STATE
CONV_DIM = D_INNER + 2 * GN
ATT_HEADS = 16
ATT_HEAD_DIM = 128
ATT_WIDTH = 2048
PREV_CHUNKS = 8
BAND_ROWS = PREV_CHUNKS * CHUNK
REL_CLIP = 128
D_FF = 4 * D_MODEL
DEPTH = 1
ALPHA = (2 * DEPTH) ** 0.25
LN_EPS = 1e-5
RMS_EPS = 1e-5
NEG_INF = -1e30

R_OFF_XBC = D_INNER
R_OFF_DT = R_OFF_XBC + CONV_DIM
R_OFF_Q = R_OFF_DT + SSD_HEADS

P_OFF_XBC = D_INNER
P_OFF_Q = P_OFF_XBC + CONV_DIM
P_OFF_K = P_OFF_Q + ATT_WIDTH
P_OFF_V = P_OFF_K + ATT_WIDTH
P_OFF_GS = P_OFF_V + ATT_WIDTH
P_OFF_GA = P_OFF_GS + D_MODEL
P_WIDTH = P_OFF_GA + D_MODEL
DT_PAD = 128

VMEM_LIMIT = 56 * 1024 * 1024


def _cparams(sem):
    return pltpu.CompilerParams(dimension_semantics=sem, vmem_limit_bytes=VMEM_LIMIT)


def _pick(n, pref, mult=8):
    best = None
    for d in range(mult, min(n, pref) + 1, mult):
        if n % d == 0:
            best = d
    assert best is not None, (n, pref, mult)
    return best


def _sigmoid(x):
    return 1.0 / (1.0 + jnp.exp(-x))


def _layer_norm(t, g, b):
    mu = jnp.mean(t, axis=-1, keepdims=True)
    tc = t - mu
    var = jnp.mean(tc * tc, axis=-1, keepdims=True)
    return tc * lax.rsqrt(var + LN_EPS) * g + b


def _inproj_kernel(x_ref, w_ref, o_ref, *, bn, gate_lo):
    j = pl.program_id(0)
    acc = jnp.dot(x_ref[...], w_ref[...], preferred_element_type=F32)
    is_gate = j * bn >= gate_lo

    @pl.when(is_gate)
    def _():
        o_ref[...] = _sigmoid(acc)

    @pl.when(jnp.logical_not(is_gate))
    def _():
        o_ref[...] = acc


def _inproj(x_bf, w_bf, gate_lo):
    T, K = x_bf.shape
    N = w_bf.shape[1]
    bm = _pick(T, 1024)
    bn = _pick(N, 1024, 128)
    return pl.pallas_call(
        functools.partial(_inproj_kernel, bn=bn, gate_lo=gate_lo),
        grid=(N // bn, T // bm),
        in_specs=[pl.BlockSpec((bm, K), lambda j, i: (i, 0)),
                  pl.BlockSpec((K, bn), lambda j, i: (0, j))],
        out_specs=pl.BlockSpec((bm, bn), lambda j, i: (i, j)),
        out_shape=jax.ShapeDtypeStruct((T, N), F32),
        compiler_params=_cparams(("parallel", "parallel")),
        name="in_proj",
    )(x_bf, w_bf)


def _split3(v):
    hi = v.astype(BF16).astype(F32)
    r = v - hi
    mid = r.astype(BF16).astype(F32)
    lo = (r - mid).astype(BF16).astype(F32)
    return hi, mid, lo


def _conv_silu(pad, cur, w_ref, b_ref):
    pad[8:8 + CHUNK, :] = cur
    acc = b_ref[...] + w_ref[0:1, :] * pad[5:5 + CHUNK, :]
    acc = acc + w_ref[1:2, :] * pad[6:6 + CHUNK, :]
    acc = acc + w_ref[2:3, :] * pad[7:7 + CHUNK, :]
    acc = acc + w_ref[3:4, :] * pad[8:8 + CHUNK, :]
    return acc * _sigmoid(acc)


def _ssd_kernel(z_ref, xs_ref, b_ref, c_ref, dt_ref, cprev_ref, sprev_ref,
                cwx_ref, cwb_ref, cwc_ref, cbx_ref, cbb_ref, cbc_ref,
                dtb_ref, alog_ref, dsk_ref, nw_ref, sel_ref,
                y_ref, cnew_ref, snew_ref,
                xpad, bpad, cpad, st):
    c = pl.program_id(1)
    nc = pl.num_programs(1)
    Q, H, P, N = CHUNK, SSD_HEADS, SSD_HEAD_DIM, SSD_STATE

    @pl.when(c == 0)
    def _():
        xpad[0:8, :] = cprev_ref[0, :, 0:D_INNER]
        bpad[0:8, :] = cprev_ref[0, :, D_INNER:D_INNER + GN]
        cpad[0:8, :] = cprev_ref[0, :, D_INNER + GN:CONV_DIM]
        st[...] = sprev_ref[0].T

    xs = _conv_silu(xpad, xs_ref[0], cwx_ref, cbx_ref)
    Bc = _conv_silu(bpad, b_ref[0], cwb_ref, cbb_ref)
    Cc = _conv_silu(cpad, c_ref[0], cwc_ref, cbc_ref)

    @pl.when(c == nc - 1)
    def _():
        cnew_ref[0, :, 0:D_INNER] = xpad[5 + Q:8 + Q, :]
        cnew_ref[0, :, D_INNER:D_INNER + GN] = bpad[5 + Q:8 + Q, :]
        cnew_ref[0, :, D_INNER + GN:CONV_DIM] = cpad[5 + Q:8 + Q, :]

    xpad[0:8, :] = xpad[Q:Q + 8, :]
    bpad[0:8, :] = bpad[Q:Q + 8, :]
    cpad[0:8, :] = cpad[Q:Q + 8, :]

    dt_in = dt_ref[0][:, 0:H] + dtb_ref[:, 0:H]
    dtv = jnp.maximum(dt_in, 0.0) + jnp.log1p(jnp.exp(-jnp.abs(dt_in)))
    A = -jnp.exp(alog_ref[:, 0:H])
    dA = dtv * A
    ri = lax.broadcasted_iota(jnp.int32, (Q, 3 * Q), 0)
    ti = lax.broadcasted_iota(jnp.int32, (Q, 3 * Q), 1) & (Q - 1)
    tril3 = jnp.where(ti <= ri, 1.0, 0.0).astype(BF16)
    rhs3 = jnp.concatenate(_split3(dA), axis=0).astype(BF16)
    cum = jnp.dot(tril3, rhs3, preferred_element_type=F32)

    lhs = jnp.concatenate([cum, dtv], axis=0)
    lhs3 = jnp.concatenate(_split3(lhs), axis=1).astype(BF16)
    e = jnp.dot(lhs3, sel_ref[...], preferred_element_type=F32)
    a = e[0:Q]
    dte = e[Q:2 * Q]
    a_last = a[Q - 1:Q, :]
    ea = jnp.exp(a)
    ea_last = ea[Q - 1:Q, :]
    xdt = dte * xs
    xe = jnp.exp(a_last - a) * xdt

    W4 = 4 * P
    row = lax.broadcasted_iota(jnp.int32, (Q, W4), 0)
    jj = lax.broadcasted_iota(jnp.int32, (Q, W4), 1) & (P - 1)
    eye4 = row == jj
    tri4 = row >= jj
    bmask = (lax.broadcasted_iota(jnp.int32, (W4, W4), 0) >> 6) == (
        lax.broadcasted_iota(jnp.int32, (W4, W4), 1) >> 6)

    for g in range(SSD_GROUPS):
        gs = slice(g * 8 * P, (g + 1) * 8 * P)
        ns = slice(g * N, (g + 1) * N)
        Bg = Bc[:, ns]
        Cg = Cc[:, ns].astype(BF16)
        Bg_bf = Bg.astype(BF16)
        Bt = jnp.concatenate([Bg_bf, Bg_bf], axis=0)
        cb2 = lax.dot_general(Cg, Bt, (((1,), (1,)), ((), ())), preferred_element_type=F32)
        cb4 = jnp.concatenate([cb2, cb2], axis=1)
        yd = []
        for half in range(2):
            sl = slice(g * 8 * P + half * W4, g * 8 * P + (half + 1) * W4)
            a4 = a[:, sl]
            b4 = jnp.sum(jnp.where(eye4, a4, 0.0), axis=0, keepdims=True)
            decay = jnp.exp(jnp.where(tri4, a4 - b4, -jnp.inf))
            m4 = (cb4 * decay).astype(BF16)
            x4 = xdt[:, sl]
            xt = jnp.concatenate([x4, x4, x4, x4], axis=0)
            bd = jnp.where(bmask, xt, 0.0).astype(BF16)
            yd.append(jnp.dot(m4, bd, preferred_element_type=F32))
        y_diag = jnp.concatenate(yd, axis=1)
        st_g = st[:, gs]
        y_off = jnp.dot(Cg, st_g.astype(BF16), preferred_element_type=F32) * ea[:, gs]
        upd = jnp.dot(Bg.T.astype(BF16), xe[:, gs].astype(BF16), preferred_element_type=F32)
        st[:, gs] = st_g * ea_last[:, gs] + upd

        xg = xs[:, gs]
        y = y_diag + y_off + dsk_ref[:, gs] * xg
        zg = z_ref[0, :, gs]
        y = y * (zg * _sigmoid(zg))
        ms = jnp.mean(y * y, axis=-1, keepdims=True)
        y = y * lax.rsqrt(ms + RMS_EPS) * nw_ref[:, gs]
        y_ref[0, :, gs] = y.astype(y_ref.dtype)

    @pl.when(c == nc - 1)
    def _():
        snew_ref[0] = st[...].T


def _ssd(proj3, dtp3, conv_prev8, ssd_prev2, cw, cb, dtb, alog, dsk_e, nw, sel):
    b, L, _ = proj3.shape
    Q = CHUNK
    nc = L // Q
    row_spec = lambda w, blk: pl.BlockSpec((1, Q, w), lambda bi, c, blk=blk: (bi, c, blk))
    full2 = lambda arr: pl.BlockSpec(arr.shape, lambda bi, c: (0, 0))
    cwx, cwb, cwc = cw[:, :D_INNER], cw[:, D_INNER:D_INNER + GN], cw[:, D_INNER + GN:]
    cbx, cbb, cbc = cb[:, :D_INNER], cb[:, D_INNER:D_INNER + GN], cb[:, D_INNER + GN:]
    small = (cwx, cwb, cwc, cbx, cbb, cbc, dtb, alog, dsk_e, nw, sel)
    return pl.pallas_call(
        _ssd_kernel,
        grid=(b, nc),
        in_specs=[row_spec(D_INNER, 0),
                  row_spec(D_INNER, P_OFF_XBC // D_INNER),
                  row_spec(GN, (P_OFF_XBC + D_INNER) // GN),
                  row_spec(GN, (P_OFF_XBC + D_INNER + GN) // GN),
                  pl.BlockSpec((1, Q, DT_PAD), lambda bi, c: (bi, c, 0)),
                  pl.BlockSpec((1, 8, CONV_DIM), lambda bi, c: (bi, 0, 0)),
                  pl.BlockSpec((1, D_INNER, SSD_STATE), lambda bi, c: (bi, 0, 0))]
                 + [full2(s) for s in small],
        out_specs=[pl.BlockSpec((1, Q, D_INNER), lambda bi, c: (bi, c, 0)),
                   pl.BlockSpec((1, CONV_WIDTH - 1, CONV_DIM), lambda bi, c: (bi, 0, 0)),
                   pl.BlockSpec((1, D_INNER, SSD_STATE), lambda bi, c: (bi, 0, 0))],
        out_shape=[jax.ShapeDtypeStruct((b, L, D_INNER), BF16),
                   jax.ShapeDtypeStruct((b, CONV_WIDTH - 1, CONV_DIM), F32),
                   jax.ShapeDtypeStruct((b, D_INNER, SSD_STATE), F32)],
        scratch_shapes=[pltpu.VMEM((8 + Q, D_INNER), F32),
                        pltpu.VMEM((8 + Q, GN), F32),
                        pltpu.VMEM((8 + Q, GN), F32),
                        pltpu.VMEM((SSD_STATE, D_INNER), F32)],
        compiler_params=_cparams(("parallel", "arbitrary")),
        name="ssd_mixer",
    )(proj3, proj3, proj3, proj3, dtp3, conv_prev8, ssd_prev2, *small)


def _bias_kernel(tab_ref, o_ref, *, G):
    h = pl.program_id(0)
    Cq = CHUNK
    ii = lax.broadcasted_iota(jnp.int32, (Cq, Cq), 0)
    jj = lax.broadcasted_iota(jnp.int32, (Cq, Cq), 1)
    far = jnp.full((Cq, Cq), tab_ref[h, 2 * REL_CLIP], F32)
    masked = jnp.full((Cq, Cq), NEG_INF, F32)
    tiles = {}
    for k in range(PREV_CHUNKS + 1):
        dmin = (PREV_CHUNKS - k) * Cq - (Cq - 1)
        dmax = (PREV_CHUNKS - k) * Cq + (Cq - 1)
        if dmin >= REL_CLIP:
            tiles[k] = far
            continue
        idx = jnp.clip((PREV_CHUNKS - k) * Cq + ii - jj, -REL_CLIP, REL_CLIP) + REL_CLIP
        lo = max(dmin, -REL_CLIP) + REL_CLIP
        hi = min(dmax, REL_CLIP) + REL_CLIP

        def body(m, acc, idx=idx):
            return jnp.where(idx == m, tab_ref[h, m], acc)

        tiles[k] = lax.fori_loop(lo, hi + 1, body, jnp.zeros((Cq, Cq), F32))
    for r in range(G):
        for w in range(G + PREV_CHUNKS):
            k = w - r
            t = tiles[k] if 0 <= k <= PREV_CHUNKS else masked
            o_ref[0, r * Cq:(r + 1) * Cq, w * Cq:(w + 1) * Cq] = t


def _rel_bias(table, G):
    GQ, WK = G * CHUNK, (G + PREV_CHUNKS) * CHUNK
    return pl.pallas_call(
        functools.partial(_bias_kernel, G=G),
        grid=(ATT_HEADS,),
        in_specs=[pl.BlockSpec(memory_space=pltpu.SMEM)],
        out_specs=pl.BlockSpec((1, GQ, WK), lambda h: (h, 0, 0)),
        out_shape=jax.ShapeDtypeStruct((ATT_HEADS, GQ, WK), F32),
        compiler_params=_cparams(("parallel",)),
        name="rel_bias",
    )(table)


def _attn_kernel(*refs, G, nkb, pre_blocks):
    q_ref = refs[0]
    k_refs = refs[1:1 + nkb]
    v_refs = refs[1 + nkb:1 + 2 * nkb]
    bias_ref = refs[1 + 2 * nkb]
    o_ref = refs[2 + 2 * nkb]
    g = pl.program_id(1)
    GQ, WK = G * CHUNK, (G + PREV_CHUNKS) * CHUNK
    first_valid = -(g + pre_blocks - (nkb - 1)) * G
    valid = (lax.broadcasted_iota(jnp.int32, (GQ, WK), 1) >> 6) >= first_valid
    scale = ATT_HEAD_DIM ** -0.5
    for h in range(ATT_HEADS):
        hs = slice(h * ATT_HEAD_DIM, (h + 1) * ATT_HEAD_DIM)
        qh = q_ref[0, :, hs].astype(BF16)
        kh = jnp.concatenate([kr[0, :, hs] for kr in k_refs], axis=0).astype(BF16)
        vh = jnp.concatenate([vr[0, :, hs] for vr in v_refs], axis=0).astype(BF16)
        s = lax.dot_general(qh, kh, (((1,), (1,)), ((), ())), preferred_element_type=F32)
        s = s * scale + bias_ref[h]
        s = jnp.where(valid, s, NEG_INF)
        m = jnp.max(s, axis=-1, keepdims=True)
        p = jnp.exp(s - m)
        l = jnp.sum(p, axis=-1, keepdims=True)
        o = jnp.dot(p.astype(BF16), vh, preferred_element_type=F32) / l
        o_ref[0, :, hs] = o.astype(o_ref.dtype)


def _attention(q3, q_blk, k3, k_blk, v3, v_blk, bias, G, pre_blocks):
    b, Lq, _ = q3.shape
    GQ, WK = G * CHUNK, (G + PREV_CHUNKS) * CHUNK
    nkb = PREV_CHUNKS // G + 1

    def kv_spec(t, blk):
        return pl.BlockSpec(
            (1, GQ, ATT_WIDTH),
            lambda bi, g, t=t, blk=blk: (bi, jnp.maximum(g + pre_blocks - (nkb - 1) + t, 0), blk))

    in_specs = ([pl.BlockSpec((1, GQ, ATT_WIDTH), lambda bi, g: (bi, g, q_blk))]
                + [kv_spec(t, k_blk) for t in range(nkb)]
                + [kv_spec(t, v_blk) for t in range(nkb)]
                + [pl.BlockSpec((ATT_HEADS, GQ, WK), lambda bi, g: (0, 0, 0),
                                pipeline_mode=pl.Buffered(1))])
    return pl.pallas_call(
        functools.partial(_attn_kernel, G=G, nkb=nkb, pre_blocks=pre_blocks),
        grid=(b, Lq // GQ),
        in_specs=in_specs,
        out_specs=pl.BlockSpec((1, GQ, ATT_WIDTH), lambda bi, g: (bi, g, 0)),
        out_shape=jax.ShapeDtypeStruct((b, Lq, ATT_WIDTH), BF16),
        compiler_params=_cparams(("parallel", "parallel")),
        name="band_attention",
    )(q3, *([k3] * nkb), *([v3] * nkb), bias)


def _merge_kernel(ys_ref, ya_ref, w1_ref, w2_ref, gs_ref, ga_ref, o_ref):
    a = jnp.dot(ys_ref[...], w1_ref[...], preferred_element_type=F32)
    b = jnp.dot(ya_ref[...], w2_ref[...], preferred_element_type=F32)
    o_ref[...] = (gs_ref[...] * a + ga_ref[...] * b).astype(o_ref.dtype)


def _merge(y_ssd, y_att, w1, w2, proj):
    T = y_ssd.shape[0]
    bm = _pick(T, 512)
    bn = 1024
    nb = D_MODEL // bn
    return pl.pallas_call(
        _merge_kernel,
        grid=(nb, T // bm),
        in_specs=[pl.BlockSpec((bm, D_INNER), lambda j, i: (i, 0)),
                  pl.BlockSpec((bm, ATT_WIDTH), lambda j, i: (i, 0)),
                  pl.BlockSpec((D_INNER, bn), lambda j, i: (0, j)),
                  pl.BlockSpec((ATT_WIDTH, bn), lambda j, i: (0, j)),
                  pl.BlockSpec((bm, bn), lambda j, i: (i, P_OFF_GS // bn + j)),
                  pl.BlockSpec((bm, bn), lambda j, i: (i, P_OFF_GA // bn + j))],
        out_specs=pl.BlockSpec((bm, bn), lambda j, i: (i, j)),
        out_shape=jax.ShapeDtypeStruct((T, D_MODEL), BF16),
        compiler_params=_cparams(("parallel", "parallel")),
        name="gated_merge",
    )(y_ssd, y_att, w1, w2, proj, proj)


def _hln_kernel(m_ref, x_ref, w_ref, g_ref, b_ref, o_ref):
    t = ALPHA * x_ref[...] + jnp.dot(m_ref[...], w_ref[...], preferred_element_type=F32)
    o_ref[...] = _layer_norm(t, g_ref[...], b_ref[...])


def _hln(merged, x2, w_out, g, bta):
    T = merged.shape[0]
    bm = _pick(T, 512)
    return pl.pallas_call(
        _hln_kernel,
        grid=(T // bm,),
        in_specs=[pl.BlockSpec((bm, D_MODEL), lambda i: (i, 0)),
                  pl.BlockSpec((bm, D_MODEL), lambda i: (i, 0)),
                  pl.BlockSpec((D_MODEL, D_MODEL), lambda i: (0, 0)),
                  pl.BlockSpec((1, D_MODEL), lambda i: (0, 0)),
                  pl.BlockSpec((1, D_MODEL), lambda i: (0, 0))],
        out_specs=pl.BlockSpec((bm, D_MODEL), lambda i: (i, 0)),
        out_shape=jax.ShapeDtypeStruct((T, D_MODEL), F32),
        compiler_params=_cparams(("parallel",)),
        name="out_proj_ln",
    )(merged, x2, w_out, g, bta)


def _mlp_kernel(h_ref, wu_ref, wd_ref, g_ref, b_ref, o_ref, hbf):
    f = pl.program_id(1)
    nf = pl.num_programs(1)

    @pl.when(f == 0)
    def _():
        hbf[...] = h_ref[...].astype(BF16)

    u = jnp.dot(hbf[...], wu_ref[...], preferred_element_type=F32)
    act = jnp.square(jnp.maximum(u, 0.0)).astype(BF16)
    d = jnp.dot(act, wd_ref[...], preferred_element_type=F32)

    @pl.when(f == 0)
    def _():
        o_ref[...] = d

    @pl.when(f > 0)
    def _():
        o_ref[...] += d

    @pl.when(f == nf - 1)
    def _():
        o_ref[...] = _layer_norm(ALPHA * h_ref[...] + o_ref[...], g_ref[...], b_ref[...])


def _mlp(h, w_up, w_down, g, bta):
    T = h.shape[0]
    bm = _pick(T, 512)
    bf = 1024
    return pl.pallas_call(
        _mlp_kernel,
        grid=(T // bm, D_FF // bf),
        in_specs=[pl.BlockSpec((bm, D_MODEL), lambda i, f: (i, 0)),
                  pl.BlockSpec((D_MODEL, bf), lambda i, f: (0, f)),
                  pl.BlockSpec((bf, D_MODEL), lambda i, f: (f, 0)),
                  pl.BlockSpec((1, D_MODEL), lambda i, f: (0, 0)),
                  pl.BlockSpec((1, D_MODEL), lambda i, f: (0, 0))],
        out_specs=pl.BlockSpec((bm, D_MODEL), lambda i, f: (i, 0)),
        out_shape=jax.ShapeDtypeStruct((T, D_MODEL), F32),
        scratch_shapes=[pltpu.VMEM((bm, D_MODEL), BF16)],
        compiler_params=_cparams(("parallel", "arbitrary")),
        name="mlp",
    )(h, w_up, w_down, g, bta)


def _layer(x, k_cache, v_cache, conv_prev, ssd_prev, P):
    b, L, _ = x.shape
    T = b * L
    x2 = x.reshape(T, D_MODEL)
    x_bf = x2.astype(BF16)

    proj = _inproj(x_bf, P["w_main"], P_OFF_GS)
    dtp = _inproj(x_bf, P["w_dt"], DT_PAD)
    proj3 = proj.reshape(b, L, P_WIDTH)
    dtp3 = dtp.reshape(b, L, DT_PAD)

    if conv_prev is None:
        conv_prev = jnp.zeros((b, CONV_WIDTH - 1, CONV_DIM), F32)
        ssd_prev = jnp.zeros((b, SSD_HEADS, SSD_HEAD_DIM, SSD_STATE), F32)
    conv_prev8 = jnp.pad(conv_prev.astype(F32), ((0, 0), (8 - (CONV_WIDTH - 1), 0), (0, 0)))
    ssd_prev2 = ssd_prev.astype(F32).reshape(b, D_INNER, SSD_STATE)
    y_ssd, conv_new, ssd_new = _ssd(proj3, dtp3, conv_prev8, ssd_prev2, P["conv_w"], P["conv_b"],
                                    P["dt_bias"], P["a_log"], P["d_skip_e"], P["norm_w"], P["sel"])

    qb, kb, vb = P_OFF_Q // ATT_WIDTH, P_OFF_K // ATT_WIDTH, P_OFF_V // ATT_WIDTH
    k_new = proj3[:, :, P_OFF_K:P_OFF_V]
    v_new = proj3[:, :, P_OFF_V:P_OFF_GS]
    if k_cache is None:
        G = 4 if L % (4 * CHUNK) == 0 else 1
        y_att = _attention(proj3, qb, proj3, kb, proj3, vb, _rel_bias(P["rel_table"], G), G, 0)
        rows = min(BAND_ROWS, L)
        k_out, v_out = k_new[:, L - rows:], v_new[:, L - rows:]
    else:
        R = k_cache.shape[1]
        assert R == BAND_ROWS and L == CHUNK
        k_full = jnp.concatenate([k_cache.reshape(b, R, ATT_WIDTH).astype(F32), k_new], axis=1)
        v_full = jnp.concatenate([v_cache.reshape(b, R, ATT_WIDTH).astype(F32), v_new], axis=1)
        y_att = _attention(proj3, qb, k_full, 0, v_full, 0, _rel_bias(P["rel_table"], 1), 1,
                           PREV_CHUNKS)
        k_out, v_out = k_new, v_new

    merged = _merge(y_ssd.reshape(T, D_INNER), y_att.reshape(T, ATT_WIDTH),
                    P["w_ssd_out"], P["w_att_out"], proj)
    h = _hln(merged, x2, P["w_out"], P["ln1_g"], P["ln1_b"])
    out = _mlp(h, P["w_up"], P["w_down"], P["ln2_g"], P["ln2_b"])

    hd = (ATT_HEADS, ATT_HEAD_DIM)
    return (out.reshape(b, L, D_MODEL),
            k_out.reshape(k_out.shape[:2] + hd), v_out.reshape(v_out.shape[:2] + hd),
            conv_new, ssd_new.reshape(b, SSD_HEADS, SSD_HEAD_DIM, SSD_STATE))


def _prep_params(w_in, conv_w, conv_b, dt_bias, a_log, d_skip, ssd_norm_w, rel_table, w_ssd_out,
                 w_att_out, w_out, ln1_g, ln1_b, w_up, w_down, ln2_g, ln2_b):
    w_main = jnp.concatenate([w_in[:, :R_OFF_DT], w_in[:, R_OFF_Q:]], axis=1).astype(BF16)
    w_dt = jnp.pad(w_in[:, R_OFF_DT:R_OFF_Q], ((0, 0), (0, DT_PAD - SSD_HEADS))).astype(BF16)
    pad_h = lambda v: jnp.pad(v.astype(F32), (0, DT_PAD - SSD_HEADS)).reshape(1, DT_PAD)
    kk = jnp.arange(3 * SSD_HEADS)[:, None] % SSD_HEADS
    hh = jnp.arange(D_INNER)[None, :] // SSD_HEAD_DIM
    return dict(
        w_main=w_main, w_dt=w_dt,
        conv_w=conv_w.astype(F32), conv_b=conv_b.astype(F32).reshape(1, CONV_DIM),
        dt_bias=pad_h(dt_bias), a_log=pad_h(a_log),
        d_skip_e=jnp.repeat(d_skip.astype(F32), SSD_HEAD_DIM).reshape(1, D_INNER),
        norm_w=ssd_norm_w.astype(F32).reshape(1, D_INNER),
        sel=(kk == hh).astype(BF16),
        rel_table=rel_table.astype(F32),
        w_ssd_out=w_ssd_out.astype(BF16), w_att_out=w_att_out.astype(BF16),
        w_out=w_out.astype(BF16),
        ln1_g=ln1_g.astype(F32).reshape(1, D_MODEL), ln1_b=ln1_b.astype(F32).reshape(1, D_MODEL),
        w_up=w_up.astype(BF16), w_down=w_down.astype(BF16),
        ln2_g=ln2_g.astype(F32).reshape(1, D_MODEL), ln2_b=ln2_b.astype(F32).reshape(1, D_MODEL),
    )


def kernel(x_prompt, x_sample, cache_k, cache_v, state_conv, state_ssd, w_in, conv_w, conv_b, dt_bias, a_log, d_skip, ssd_norm_w, rel_table, w_ssd_out, w_att_out, w_out, ln1_g, ln1_b, w_up, w_down, ln2_g, ln2_b):
    hp, hs = x_prompt, x_sample
    outs_p, outs_s = [], []
    for l in range(w_in.shape[0]):
        P = _prep_params(w_in[l], conv_w[l], conv_b[l], dt_bias[l], a_log[l], d_skip[l],
                         ssd_norm_w[l], rel_table[l], w_ssd_out[l], w_att_out[l], w_out[l],
                         ln1_g[l], ln1_b[l], w_up[l], w_down[l], ln2_g[l], ln2_b[l])
        hp, *rest_p = _layer(hp, None, None, None, None, P)
        hs, *rest_s = _layer(hs, cache_k[l], cache_v[l], state_conv[l], state_ssd[l], P)
        outs_p.append(rest_p)
        outs_s.append(rest_s)
    stack = lambda outs, i: jnp.stack([o[i] for o in outs])
    return (hp, hs,
            stack(outs_p, 0), stack(outs_p, 1), stack(outs_p, 2), stack(outs_p, 3),
            stack(outs_s, 0), stack(outs_s, 1), stack(outs_s, 2), stack(outs_s, 3))
```
